```python
import math
import jax
import jax.numpy as jnp
from jax import lax
import numpy as np


D_MODEL = 1024
BATCH = 16
SEQ = 4096
DEPTH = 1

CONV_WIDTH = 4
RMS_EPS = 1e-6
LRU_WIDTH = D_MODEL
LRU_HEADS = 16
LRU_HEAD_DIM = LRU_WIDTH // LRU_HEADS
LRU_C = 8.0
SSD_INNER = 2 * D_MODEL
SSD_HEAD_DIM = 64
SSD_HEADS = SSD_INNER // SSD_HEAD_DIM
SSD_GROUPS = 4
SSD_HEADS_PER_GROUP = SSD_HEADS // SSD_GROUPS
SSD_STATE = 128
SSD_CHUNK = 128
SSD_CONV_DIM = SSD_INNER + 2 * SSD_GROUPS * SSD_STATE
PEER_HEADS = 8
PEER_KEYS = 128
PEER_EXPERTS = PEER_KEYS * PEER_KEYS
PEER_TOPK = 16
PEER_QUERY_DIM = 256
PEER_HALF = PEER_QUERY_DIM // 2
PEER_TOKEN_BLOCK = 128
IN_DIM = LRU_WIDTH + SSD_INNER + SSD_CONV_DIM + SSD_HEADS + 2 * D_MODEL

kernel_name = 'hybrid_rglru_ssd_peer_block'


def rms_norm(x, w):
    xf = x.astype(jnp.float32)
    y = xf * lax.rsqrt(jnp.mean(xf * xf, axis=-1, keepdims=True) + RMS_EPS)
    return (y * w.astype(jnp.float32)).astype(x.dtype)


def causal_depthwise_conv(x, w, b):
    y = lax.conv_general_dilated(
        x, w[:, None, :], window_strides=(1,), padding=[(CONV_WIDTH - 1, 0)],
        dimension_numbers=('NWC', 'WIO', 'NWC'), feature_group_count=x.shape[-1])
    return y + b


def rg_lru(x, w_a, b_a, w_x, b_x, lam):
    bsz, seqlen, _ = x.shape
    xh = x.reshape(bsz, seqlen, LRU_HEADS, LRU_HEAD_DIM)
    r = jax.nn.sigmoid(jnp.einsum('blhi,hij->blhj', xh, w_a).reshape(bsz, seqlen, LRU_WIDTH) + b_a)
    i = jax.nn.sigmoid(jnp.einsum('blhi,hij->blhj', xh, w_x).reshape(bsz, seqlen, LRU_WIDTH) + b_x)
    log_a = -LRU_C * r.astype(jnp.float32) * jax.nn.softplus(-lam.astype(jnp.float32))
    a = jnp.exp(log_a)
    u = jnp.sqrt(-jnp.expm1(2.0 * log_a)) * (i * x).astype(jnp.float32)

    def step(h, au):
        a_t, u_t = au
        h = a_t * h + u_t
        return h, h

    h0 = jnp.zeros((bsz, LRU_WIDTH), jnp.float32)
    _, hs = lax.scan(step, h0, (jnp.swapaxes(a, 0, 1), jnp.swapaxes(u, 0, 1)))
    return jnp.swapaxes(hs, 0, 1).astype(x.dtype)


def ssd_mixer(xbc, z, dt_raw, conv_w, conv_b, dt_bias, a_log, d_skip, norm_w):
    dtype = xbc.dtype
    f32 = jnp.float32
    bsz, seqlen, _ = xbc.shape
    G, HG, P, N, Q = SSD_GROUPS, SSD_HEADS_PER_GROUP, SSD_HEAD_DIM, SSD_STATE, SSD_CHUNK
    nc = seqlen // Q
    xbc = jax.nn.silu(causal_depthwise_conv(xbc, conv_w, conv_b)).astype(f32)
    xs, bm, cm = jnp.split(xbc, [SSD_INNER, SSD_INNER + G * N], axis=-1)
    xs = xs.reshape(bsz, nc, Q, G, HG, P)
    bm = bm.reshape(bsz, nc, Q, G, N)
    cm = cm.reshape(bsz, nc, Q, G, N)
    dt = jax.nn.softplus(dt_raw.astype(f32) + dt_bias.astype(f32))
    a = -jnp.exp(a_log.astype(f32))
    da = (dt * a).reshape(bsz, nc, Q, G, HG).transpose(0, 3, 4, 1, 2)
    xdt = xs * dt.reshape(bsz, nc, Q, G, HG)[..., None]
    a_cs = jnp.cumsum(da, axis=-1)
    causal = jnp.tril(jnp.ones((Q, Q), dtype=bool))
    decay_ls = jnp.exp(jnp.where(causal, a_cs[..., :, None] - a_cs[..., None, :], -jnp.inf))
    cb = jnp.einsum('bclgn,bcsgn->bgcls', cm, bm)
    y_diag = jnp.einsum('bghcls,bcsghp->bclghp', cb[:, :, None] * decay_ls, xdt)
    decay_states = jnp.exp(a_cs[..., -1:] - a_cs).transpose(0, 3, 4, 1, 2)
    states = jnp.einsum('bclgn,bclghp->bcghpn', bm, xdt * decay_states[..., None])
    chunk_decay = jnp.exp(a_cs[..., -1])

    def step(s, inp):
        dec, st = inp
        return dec[..., None, None] * s + st, s

    s0 = jnp.zeros((bsz, G, HG, P, N), f32)
    _, prev = lax.scan(step, s0, (jnp.moveaxis(chunk_decay, -1, 0), jnp.moveaxis(states, 1, 0)))
    prev = jnp.moveaxis(prev, 0, 1)
    decay_out = jnp.exp(a_cs).transpose(0, 3, 4, 1, 2)
    y_off = jnp.einsum('bclgn,bcghpn->bclghp', cm, prev) * decay_out[..., None]
    y = y_diag + y_off + xs * d_skip.astype(f32).reshape(G, HG)[:, :, None]
    y = y.reshape(bsz, seqlen, SSD_INNER)
    yg = (y * jax.nn.silu(z.astype(f32))).reshape(bsz, seqlen, G, SSD_INNER // G)
    y = rms_norm(yg, norm_w.reshape(G, SSD_INNER // G)).reshape(bsz, seqlen, SSD_INNER)
    return y.astype(dtype)


def peer_ffn(h, w_q, sub_keys, expert_down, expert_up):
    tokens = h.reshape(-1, D_MODEL)
    nblk = tokens.shape[0] // PEER_TOKEN_BLOCK
    K = PEER_TOPK

    def block(xb):
        t = xb.shape[0]
        q = (xb @ w_q).reshape(t, PEER_HEADS, 2, PEER_HALF)
        s = jnp.einsum('thpd,hpkd->thpk', q, sub_keys).astype(jnp.float32)
        sv, si = lax.top_k(s, K)
        cand = sv[:, :, 0, :, None] + sv[:, :, 1, None, :]
        cv, ci = lax.top_k(cand.reshape(t, PEER_HEADS, K * K), K)
        i1 = jnp.take_along_axis(si[:, :, 0, :], ci // K, axis=-1)
        i2 = jnp.take_along_axis(si[:, :, 1, :], ci % K, axis=-1)
        idx = i1 * PEER_KEYS + i2
        g = jax.nn.softmax(cv, axis=-1)
        u = expert_down[idx]
        act = jax.nn.gelu(jnp.einsum('thkd,td->thk', u, xb).astype(jnp.float32), approximate=False)
        v = expert_up[idx]
        return jnp.einsum('thk,thkd->td', (g * act).astype(xb.dtype), v)

    out = lax.map(block, tokens.reshape(nblk, PEER_TOKEN_BLOCK, D_MODEL))
    return out.reshape(h.shape)


def setup_inputs(seed: int = 0):
    key = jax.random.key(seed)
    ks = jax.random.split(key, 32)
    f32 = jnp.float32
    L = DEPTH

    def nrm(k, shape, scale):
        return jax.random.normal(k, shape, f32) * scale

    def gain(k, shape):
        return 1.0 + 0.05 * jax.random.normal(k, shape, f32)

    x = nrm(ks[0], (BATCH, SEQ, D_MODEL), 1.0)
    c = nrm(ks[1], (BATCH, D_MODEL), 1.0)
    w_ada = nrm(ks[2], (L, D_MODEL, 6 * D_MODEL), 0.5 * D_MODEL ** -0.5)
    b_ada = nrm(ks[3], (L, 6 * D_MODEL), 0.01)
    norm_pre_mix = gain(ks[4], (L, D_MODEL))
    norm_post_mix = gain(ks[5], (L, D_MODEL))
    norm_pre_ffn = gain(ks[6], (L, D_MODEL))
    norm_post_ffn = gain(ks[7], (L, D_MODEL))
    w_in = nrm(ks[8], (L, D_MODEL, IN_DIM), D_MODEL ** -0.5)
    lru_conv_w = nrm(ks[9], (L, CONV_WIDTH, LRU_WIDTH), CONV_WIDTH ** -0.5)
    lru_conv_b = nrm(ks[10], (L, LRU_WIDTH), 0.01)
    lru_wa = nrm(ks[11], (L, LRU_HEADS, LRU_HEAD_DIM, LRU_HEAD_DIM), LRU_HEAD_DIM ** -0.5)
    lru_ba = nrm(ks[12], (L, LRU_WIDTH), 0.01)
    lru_wx = nrm(ks[13], (L, LRU_HEADS, LRU_HEAD_DIM, LRU_HEAD_DIM), LRU_HEAD_DIM ** -0.5)
    lru_bx = nrm(ks[14], (L, LRU_WIDTH), 0.01)
    a_pow = jax.random.uniform(ks[15], (L, LRU_WIDTH), f32, 0.9, 0.999)
    a_base = a_pow ** (1.0 / LRU_C)
    lru_lambda = jnp.log(a_base) - jnp.log1p(-a_base)
    ssd_conv_w = nrm(ks[16], (L, CONV_WIDTH, SSD_CONV_DIM), CONV_WIDTH ** -0.5)
    ssd_conv_b = nrm(ks[17], (L, SSD_CONV_DIM), 0.01)
    dt0 = jnp.exp(jax.random.uniform(ks[18], (L, SSD_HEADS), f32, math.log(1e-3), math.log(1e-1)))
    ssd_dt_bias = dt0 + jnp.log(-jnp.expm1(-dt0))
    ssd_a_log = jnp.log(jax.random.uniform(ks[19], (L, SSD_HEADS), f32, 1.0, 16.0))
    ssd_d = 1.0 + 0.1 * jax.random.normal(ks[20], (L, SSD_HEADS), f32)
    ssd_norm_w = gain(ks[21], (L, SSD_INNER))
    w_lru_out = nrm(ks[22], (L, LRU_WIDTH, D_MODEL), LRU_WIDTH ** -0.5)
    w_ssd_out = nrm(ks[23], (L, SSD_INNER, D_MODEL), SSD_INNER ** -0.5)
    w_o = nrm(ks[24], (L, D_MODEL, D_MODEL), D_MODEL ** -0.5)
    peer_wq = nrm(ks[25], (L, D_MODEL, PEER_HEADS * PEER_QUERY_DIM), D_MODEL ** -0.5)
    peer_sub_keys = nrm(ks[26], (L, PEER_HEADS, 2, PEER_KEYS, PEER_HALF), PEER_HALF ** -0.5)
    peer_down = nrm(ks[27], (L, PEER_EXPERTS, D_MODEL), D_MODEL ** -0.5)
    peer_up = nrm(ks[28], (L, PEER_EXPERTS, D_MODEL), (PEER_HEADS * PEER_TOPK) ** -0.5)
    return {
        'x': x, 'c': c, 'w_ada': w_ada, 'b_ada': b_ada,
        'norm_pre_mix': norm_pre_mix, 'norm_post_mix': norm_post_mix,
        'norm_pre_ffn': norm_pre_ffn, 'norm_post_ffn': norm_post_ffn,
        'w_in': w_in, 'lru_conv_w': lru_conv_w, 'lru_conv_b': lru_conv_b,
        'lru_wa': lru_wa, 'lru_ba': lru_ba, 'lru_wx': lru_wx, 'lru_bx': lru_bx,
        'lru_lambda': lru_lambda, 'ssd_conv_w': ssd_conv_w, 'ssd_conv_b': ssd_conv_b,
        'ssd_dt_bias': ssd_dt_bias, 'ssd_a_log': ssd_a_log, 'ssd_d': ssd_d,
        'ssd_norm_w': ssd_norm_w, 'w_lru_out': w_lru_out, 'w_ssd_out': w_ssd_out,
        'w_o': w_o, 'peer_wq': peer_wq, 'peer_sub_keys': peer_sub_keys,
        'peer_down': peer_down, 'peer_up': peer_up,
    }


def reference(x, c, w_ada, b_ada, norm_pre_mix, norm_post_mix, norm_pre_ffn, norm_post_ffn,
              w_in, lru_conv_w, lru_conv_b, lru_wa, lru_ba, lru_wx, lru_bx, lru_lambda,
              ssd_conv_w, ssd_conv_b, ssd_dt_bias, ssd_a_log, ssd_d, ssd_norm_w,
              w_lru_out, w_ssd_out, w_o, peer_wq, peer_sub_keys, peer_down, peer_up):
    o1 = LRU_WIDTH
    o2 = o1 + SSD_INNER
    o3 = o2 + SSD_CONV_DIM
    o4 = o3 + SSD_HEADS
    o5 = o4 + D_MODEL
    c_act = jax.nn.silu(c)
    for l in range(DEPTH):
        mod = c_act @ w_ada[l] + b_ada[l]
        shift1, scale1, gate1, shift2, scale2, gate2 = jnp.split(mod[:, None, :], 6, axis=-1)
        h = rms_norm(x, norm_pre_mix[l]) * (1.0 + scale1) + shift1
        proj = h @ w_in[l]
        lru_x, ssd_z, ssd_xbc, ssd_dt, gate_lru, gate_ssd = jnp.split(proj, [o1, o2, o3, o4, o5], axis=-1)
        y_lru = rg_lru(causal_depthwise_conv(lru_x, lru_conv_w[l], lru_conv_b[l]),
                       lru_wa[l], lru_ba[l], lru_wx[l], lru_bx[l], lru_lambda[l])
        y_ssd = ssd_mixer(ssd_xbc, ssd_z, ssd_dt, ssd_conv_w[l], ssd_conv_b[l],
                          ssd_dt_bias[l], ssd_a_log[l], ssd_d[l], ssd_norm_w[l])
        merged = (jax.nn.sigmoid(gate_lru) * (y_lru @ w_lru_out[l])
                  + jax.nn.sigmoid(gate_ssd) * (y_ssd @ w_ssd_out[l]))
        x = x + gate1 * rms_norm(merged @ w_o[l], norm_post_mix[l])
        h = rms_norm(x, norm_pre_ffn[l]) * (1.0 + scale2) + shift2
        y_ffn = peer_ffn(h, peer_wq[l], peer_sub_keys[l], peer_down[l], peer_up[l])
        x = x + gate2 * rms_norm(y_ffn, norm_post_ffn[l])
    return x
```

```python
import functools
import math

import jax
import jax.numpy as jnp
from jax import lax
from jax.experimental import pallas as pl
from jax.experimental.pallas import tpu as pltpu

F32 = jnp.float32
BF16 = jnp.bfloat16

D_MODEL = 1024
CONV_WIDTH = 4
RMS_EPS = 1e-6
LRU_WIDTH = D_MODEL
LRU_HEADS = 16
LRU_HEAD_DIM = LRU_WIDTH // LRU_HEADS
LRU_C = 8.0
SSD_INNER = 2 * D_MODEL
SSD_HEAD_DIM = 64
SSD_HEADS = SSD_INNER // SSD_HEAD_DIM
SSD_GROUPS = 4
SSD_HEADS_PER_GROUP = SSD_HEADS // SSD_GROUPS
SSD_STATE = 128
SSD_CHUNK = 128
SSD_GROUP_WIDTH = SSD_INNER // SSD_GROUPS
SSD_CONV_DIM = SSD_INNER + 2 * SSD_GROUPS * SSD_STATE
PEER_HEADS = 8
PEER_KEYS = 128
PEER_EXPERTS = PEER_KEYS * PEER_KEYS
PEER_TOPK = 16
PEER_HALF = 128

LANES = 128
SUBLANES = 8
DT_PAD = LANES
VMEM_LIMIT = 56 * 1024 * 1024

INPROJ_ROWS = 256
OUTPROJ_ROWS = 256
SCORE_TOKENS = 256
EXPERT_TOKENS = 512
EXPERT_CHUNK = 1024


def _dot(a, b):
    return jnp.dot(a, b, preferred_element_type=F32)


def _dot_nt(a, b):
    return lax.dot_general(a, b, (((1,), (1,)), ((), ())), preferred_element_type=F32)


def _dot_tn(a, b):
    return lax.dot_general(a, b, (((0,), (0,)), ((), ())), preferred_element_type=F32)


def _rms(x, w):
    ms = jnp.mean(x * x, axis=-1, keepdims=True)
    return x * lax.rsqrt(ms + RMS_EPS) * w


def _sigmoid(x):
    return 1.0 / (1.0 + jnp.exp(-x))


def _softplus(x):
    return jnp.maximum(x, 0.0) + jnp.log1p(jnp.exp(-jnp.abs(x)))


def _split3(v):
    hi = v.astype(BF16)
    r1 = v - hi.astype(F32)
    lo = r1.astype(BF16)
    r2 = r1 - lo.astype(F32)
    return jnp.concatenate([hi, lo, r2.astype(BF16)], axis=1)


def _adaln_kernel(c_ref, w_ref, b_ref, o_ref):
    c = c_ref[...]
    ca = c * _sigmoid(c)
    w = w_ref[...]
    c_hi = ca.astype(BF16)
    c_lo = (ca - c_hi.astype(F32)).astype(BF16)
    w_hi = w.astype(BF16)
    w_lo = (w - w_hi.astype(F32)).astype(BF16)
    o_ref[...] = _dot(c_hi, w_hi) + _dot(c_hi, w_lo) + _dot(c_lo, w_hi) + b_ref[...]


def _adaln(c, w_ada, b_ada):
    bsz, d = c.shape
    n = w_ada.shape[1]
    blk = D_MODEL
    return pl.pallas_call(
        _adaln_kernel,
        grid=(n // blk,),
        in_specs=[
            pl.BlockSpec((bsz, d), lambda j: (0, 0)),
            pl.BlockSpec((d, blk), lambda j: (0, j)),
            pl.BlockSpec((1, blk), lambda j: (0, j)),
        ],
        out_specs=pl.BlockSpec((bsz, blk), lambda j: (0, j)),
        out_shape=jax.ShapeDtypeStruct((bsz, n), F32),
        compiler_params=pltpu.CompilerParams(dimension_semantics=("arbitrary",)),
        name="adaln",
    )(c, w_ada, b_ada.reshape(1, n))


def _inproj_kernel(x_ref, sc_ref, sh_ref, nw_ref, w_ref, lru_ref, xbc_ref, dt_ref):
    h = _rms(x_ref[...], nw_ref[...]) * (1.0 + sc_ref[0]) + sh_ref[0]
    hb = h.astype(BF16)
    o1 = LRU_WIDTH
    o2 = o1 + SSD_CONV_DIM
    lru_ref[...] = _dot(hb, w_ref[:, 0:o1])
    xbc_ref[...] = _dot(hb, w_ref[:, o1:o2])
    dt_ref[...] = _dot(hb, w_ref[:, o2:o2 + DT_PAD])


def _inproj(x2, scale1, shift1, norm_w, w1, seq):
    t, d = x2.shape
    rows = INPROJ_ROWS
    per_seq = seq // rows
    n = w1.shape[1]
    mod_spec = pl.BlockSpec((1, 1, d), lambda i: (i // per_seq, 0, 0))
    return pl.pallas_call(
        _inproj_kernel,
        grid=(t // rows,),
        in_specs=[
            pl.BlockSpec((rows, d), lambda i: (i, 0)),
            mod_spec, mod_spec,
            pl.BlockSpec((1, d), lambda i: (0, 0)),
            pl.BlockSpec((d, n), lambda i: (0, 0)),
        ],
        out_specs=[
            pl.BlockSpec((rows, LRU_WIDTH), lambda i: (i, 0)),
            pl.BlockSpec((rows, SSD_CONV_DIM), lambda i: (i, 0)),
            pl.BlockSpec((rows, DT_PAD), lambda i: (i, 0)),
        ],
        out_shape=[
            jax.ShapeDtypeStruct((t, LRU_WIDTH), F32),
            jax.ShapeDtypeStruct((t, SSD_CONV_DIM), F32),
            jax.ShapeDtypeStruct((t, DT_PAD), F32),
        ],
        compiler_params=pltpu.CompilerParams(
            dimension_semantics=("arbitrary",), vmem_limit_bytes=VMEM_LIMIT),
        name="inproj",
    )(x2, scale1, shift1, norm_w, w1)


def _causal_conv(buf, x_ref, w_ref, b_ref):
    q = SSD_CHUNK
    buf[SUBLANES:SUBLANES + q, :] = x_ref[...]
    w = w_ref[...]
    y = b_ref[...] + w[CONV_WIDTH - 1:CONV_WIDTH] * buf[SUBLANES:SUBLANES + q, :]
    for k in range(CONV_WIDTH - 1):
        back = CONV_WIDTH - 1 - k
        y = y + w[k:k + 1] * buf[SUBLANES - back:SUBLANES - back + q, :]
    buf[0:SUBLANES, :] = buf[q:q + SUBLANES, :]
    return y


def _mixer_kernel(lru_ref, xbc_ref, dt_ref,
                  lcw_ref, lcb_ref, wg_ref, ba_ref, bx_ref, lam_ref,
                  scw_ref, scb_ref, dtb_ref, alog_ref, r64_ref, r128_ref, dx_ref,
                  ylru_ref, yssd_ref,
                  lbuf, sbuf, hst, sst):
    q = SSD_CHUNK

    @pl.when(pl.program_id(1) == 0)
    def _():
        lbuf[0:SUBLANES, :] = jnp.zeros((SUBLANES, LRU_WIDTH), F32)
        sbuf[0:SUBLANES, :] = jnp.zeros((SUBLANES, SSD_CONV_DIM), F32)
        hst[...] = jnp.zeros_like(hst)
        sst[...] = jnp.zeros_like(sst)

    row = lax.broadcasted_iota(jnp.int32, (q, LANES), 0)
    lane = lax.broadcasted_iota(jnp.int32, (q, LANES), 1)

    xc = _causal_conv(lbuf, lru_ref, lcw_ref, lcb_ref)
    blk = 4 * LRU_HEAD_DIM
    gates = [_dot(xc[:, blk * k:blk * (k + 1)].astype(BF16), wg_ref[k]) for k in range(LRU_WIDTH // blk)]
    ga = jnp.concatenate([g[:, :blk] for g in gates], axis=1) + ba_ref[...]
    gx = jnp.concatenate([g[:, blk:] for g in gates], axis=1) + bx_ref[...]
    log_a = (-LRU_C) * _sigmoid(ga) * _softplus(-lam_ref[...])
    a_all = jnp.exp(log_a)
    u_all = jnp.sqrt(1.0 - a_all * a_all) * (_sigmoid(gx) * xc)
    row8 = row & (SUBLANES - 1)
    for col in range(LRU_WIDTH // LANES):
        sl = slice(LANES * col, LANES * (col + 1))
        a = a_all[:, sl]
        u = u_all[:, sl]
        d = 1
        while d < SUBLANES:
            keep = row8 >= d
            u = jnp.where(keep, a * pltpu.roll(u, d, 0) + u, u)
            a = jnp.where(keep, a * pltpu.roll(a, d, 0), a)
            d *= 2
        carry = hst[:, sl]
        for g in range(q // SUBLANES):
            rs = slice(SUBLANES * g, SUBLANES * (g + 1))
            hg = a[rs] * carry + u[rs]
            ylru_ref[rs, sl] = hg
            carry = jnp.broadcast_to(hg[SUBLANES - 1:SUBLANES], (SUBLANES, LANES))
        hst[:, sl] = carry

    xbc = _causal_conv(sbuf, xbc_ref, scw_ref, scb_ref)
    xbc = xbc * _sigmoid(xbc)
    xs = xbc[:, :SSD_INNER]
    gn = SSD_GROUPS * SSD_STATE
    bm = xbc[:, SSD_INNER:SSD_INNER + gn].astype(BF16)
    cm = xbc[:, SSD_INNER + gn:].astype(BF16)
    dt = jnp.where(lane < SSD_HEADS, _softplus(dt_ref[...] + dtb_ref[...]), 0.0)
    acs = dt * (-jnp.exp(alog_ref[...]))
    d = 1
    while d < q:
        acs = acs + jnp.where(row >= d, pltpu.roll(acs, d, 0), 0.0)
        d *= 2
    acs3 = _split3(acs)
    acs_x = _dot(acs3, r64_ref[...])
    dt_x = _dot(_split3(dt), r64_ref[...])
    acs_w = _dot(acs3, r128_ref[...])
    acs_t = acs.T
    last_x = acs_x[q - 1:q, :]
    xdt = xs * dt_x
    xds = (xdt * jnp.exp(last_x - acs_x)).astype(BF16)
    dec_out = jnp.exp(acs_x)
    chunk_dec = jnp.exp(last_x)
    causal = row >= lane
    first_half = lane < SSD_HEAD_DIM
    gw = SSD_GROUP_WIDTH
    for g in range(SSD_GROUPS):
        bg = bm[:, SSD_STATE * g:SSD_STATE * (g + 1)]
        cg = cm[:, SSD_STATE * g:SSD_STATE * (g + 1)]
        cb = _dot_nt(cg, bg)
        prev = sst[g]
        y_off = _dot(cg, prev.astype(BF16)) * dec_out[:, gw * g:gw * (g + 1)]
        sst[g] = chunk_dec[:, gw * g:gw * (g + 1)] * prev + _dot_tn(bg, xds[:, gw * g:gw * (g + 1)])
        for pair in range(SSD_HEADS_PER_GROUP // 2):
            h0 = SSD_HEADS_PER_GROUP * g + 2 * pair
            lhs = []
            for hh in (h0, h0 + 1):
                decay = jnp.exp(acs_w[:, LANES * hh:LANES * (hh + 1)] - acs_t[hh:hh + 1, :])
                lhs.append((cb * jnp.where(causal, decay, 0.0)).astype(BF16))
            x2 = xdt[:, SSD_HEAD_DIM * h0:SSD_HEAD_DIM * h0 + LANES]
            rhs = jnp.concatenate([jnp.where(first_half, x2, 0.0).astype(BF16),
                                   jnp.where(first_half, 0.0, x2).astype(BF16)], axis=0)
            y_diag = _dot(jnp.concatenate(lhs, axis=1), rhs)
            cs = slice(SSD_HEAD_DIM * h0, SSD_HEAD_DIM * h0 + LANES)
            yssd_ref[:, cs] = y_diag + y_off[:, LANES * pair:LANES * (pair + 1)] + xs[:, cs] * dx_ref[:, cs]


def _mixer(lru_x, xbc, dt, p, bsz, seq):
    q = SSD_CHUNK
    nc = seq // q
    t = bsz * seq

    def tok(width):
        return pl.BlockSpec((q, width), lambda b, c: (b * nc + c, 0))

    def full(shape):
        nd = len(shape)
        return pl.BlockSpec(shape, lambda b, c: (0,) * nd)

    consts = [p["lru_conv_w"], p["lru_conv_b"], p["lru_wg"], p["lru_ba"], p["lru_bx"], p["lru_lambda"],
              p["ssd_conv_w"], p["ssd_conv_b"], p["ssd_dt_bias"], p["ssd_a_log"], p["r64"], p["r128"], p["ssd_dx"]]
    return pl.pallas_call(
        _mixer_kernel,
        grid=(bsz, nc),
        in_specs=[tok(LRU_WIDTH), tok(SSD_CONV_DIM), tok(DT_PAD)] + [full(a.shape) for a in consts],
        out_specs=[tok(LRU_WIDTH), tok(SSD_INNER)],
        out_shape=[jax.ShapeDtypeStruct((t, LRU_WIDTH), F32), jax.ShapeDtypeStruct((t, SSD_INNER), F32)],
        scratch_shapes=[
            pltpu.VMEM((SUBLANES + q, LRU_WIDTH), F32),
            pltpu.VMEM((SUBLANES + q, SSD_CONV_DIM), F32),
            pltpu.VMEM((SUBLANES, LRU_WIDTH), F32),
            pltpu.VMEM((SSD_GROUPS, SSD_STATE, SSD_GROUP_WIDTH), F32),
        ],
        compiler_params=pltpu.CompilerParams(
            dimension_semantics=("arbitrary", "arbitrary"), vmem_limit_bytes=VMEM_LIMIT),
        name="mixer",
    )(lru_x, xbc, dt, *consts)


def _outproj_kernel(x_ref, ylru_ref, yssd_ref, sc1_ref, sh1_ref, g1_ref, sc2_ref, sh2_ref,
                    npre_ref, npost_ref, nffn_ref, snw_ref,
                    wzg_ref, wlo_ref, wso_ref, wo_ref,
                    xmid_ref, h2t_ref):
    x = x_ref[...]
    h = _rms(x, npre_ref[...]) * (1.0 + sc1_ref[0]) + sh1_ref[0]
    hb = h.astype(BF16)
    z = _dot(hb, wzg_ref[:, 0:SSD_INNER])
    yg = yssd_ref[...] * (z * _sigmoid(z))
    gw = SSD_GROUP_WIDTH
    yn = jnp.concatenate(
        [_rms(yg[:, gw * g:gw * (g + 1)], snw_ref[:, gw * g:gw * (g + 1)]) for g in range(SSD_GROUPS)], axis=1)
    gate_lru = _sigmoid(_dot(hb, wzg_ref[:, SSD_INNER:SSD_INNER + D_MODEL]))
    gate_ssd = _sigmoid(_dot(hb, wzg_ref[:, SSD_INNER + D_MODEL:]))
    merged = (gate_lru * _dot(ylru_ref[...].astype(BF16), wlo_ref[...])
              + gate_ssd * _dot(yn.astype(BF16), wso_ref[...]))
    o = _dot(merged.astype(BF16), wo_ref[...])
    xm = x + g1_ref[0] * _rms(o, npost_ref[...])
    xmid_ref[...] = xm
    h2 = _rms(xm, nffn_ref[...]) * (1.0 + sc2_ref[0]) + sh2_ref[0]
    h2t_ref[...] = h2.T.astype(BF16)


def _outproj(x2, y_lru, y_ssd, mods, p, seq):
    t, d = x2.shape
    rows = OUTPROJ_ROWS
    per_seq = seq // rows
    mod_spec = pl.BlockSpec((1, 1, d), lambda i: (i // per_seq, 0, 0))

    def full(a):
        nd = a.ndim
        return pl.BlockSpec(a.shape, lambda i: (0,) * nd)

    consts = [p["norm_pre_mix"], p["norm_post_mix"], p["norm_pre_ffn"], p["ssd_norm_w"],
              p["w_zg"], p["w_lru_out"], p["w_ssd_out"], p["w_o"]]
    return pl.pallas_call(
        _outproj_kernel,
        grid=(t // rows,),
        in_specs=[
            pl.BlockSpec((rows, d), lambda i: (i, 0)),
            pl.BlockSpec((rows, LRU_WIDTH), lambda i: (i, 0)),
            pl.BlockSpec((rows, SSD_INNER), lambda i: (i, 0)),
        ] + [mod_spec] * 5 + [full(a) for a in consts],
        out_specs=[
            pl.BlockSpec((rows, d), lambda i: (i, 0)),
            pl.BlockSpec((d, rows), lambda i: (0, i)),
        ],
        out_shape=[jax.ShapeDtypeStruct((t, d), F32), jax.ShapeDtypeStruct((d, t), BF16)],
        compiler_params=pltpu.CompilerParams(
            dimension_semantics=("arbitrary",), vmem_limit_bytes=VMEM_LIMIT),
        name="outproj",
    )(x2, y_lru, y_ssd, *mods, *consts)


def _top_values(v, n):
    out = []
    for _ in range(n):
        m = jnp.max(v, axis=0, keepdims=True)
        out.append(m)
        v = jnp.where(v == m, -jnp.inf, v)
    return out


def _score_kernel(h2t_ref, wqt_ref, keys_ref, tau_ref, e1_ref, s2_ref, e2_ref):
    qt = _dot(wqt_ref[...], h2t_ref[...]).astype(BF16)
    k = PEER_TOPK
    for h in range(PEER_HEADS):
        base = 2 * PEER_HALF * h
        s1 = _dot(keys_ref[2 * h], qt[base:base + PEER_HALF])
        s2 = _dot(keys_ref[2 * h + 1], qt[base + PEER_HALF:base + 2 * PEER_HALF])
        rs = slice(PEER_KEYS * h, PEER_KEYS * (h + 1))
        for c in range(s1.shape[1] // LANES):
            cs = slice(LANES * c, LANES * (c + 1))
            s1c = s1[:, cs]
            s2c = s2[:, cs]
            top1 = _top_values(s1c, k)
            top2 = jnp.concatenate(_top_values(s2c, k), axis=0)
            cand = jnp.concatenate([top1[a] + top2 for a in range(k)], axis=0)
            cv = _top_values(cand, k + 1)
            z = jnp.exp(cv[1] - cv[0]) + 1.0
            for r in range(2, k):
                z = z + jnp.exp(cv[r] - cv[0])
            theta = 0.5 * (cv[k - 1] + cv[k])
            tau_ref[rs, cs] = theta - s1c
            e1_ref[rs, cs] = jnp.exp(s1c - top1[0])
            s2_ref[rs, cs] = s2c
            e2_ref[rs, cs] = jnp.exp(s2c - top2[0:1]) / z


def _score(h2t, wqt, keys):
    d, t = h2t.shape
    tb = SCORE_TOKENS
    n = PEER_HEADS * PEER_KEYS
    out = jax.ShapeDtypeStruct((n, t), F32)
    ospec = pl.BlockSpec((n, tb), lambda i: (0, i))
    return pl.pallas_call(
        _score_kernel,
        grid=(t // tb,),
        in_specs=[
            pl.BlockSpec((d, tb), lambda i: (0, i)),
            pl.BlockSpec(wqt.shape, lambda i: (0, 0)),
            pl.BlockSpec(keys.shape, lambda i: (0, 0, 0)),
        ],
        out_specs=[ospec] * 4,
        out_shape=[out] * 4,
        compiler_params=pltpu.CompilerParams(
            dimension_semantics=("arbitrary",), vmem_limit_bytes=VMEM_LIMIT),
        name="peer_score",
    )(h2t, wqt, keys)


def _gelu(x):
    return 0.5 * x * (1.0 + lax.erf(x * (1.0 / math.sqrt(2.0))))


def _expert_kernel(xmid_ref, g2_ref, nw_ref, h2t_ref, tau_ref, e1_ref, s2_ref, e2_ref,
                   down_ref, upt_ref, out_ref, acc, pbuf):
    e = pl.program_id(1)
    ec, tb = pbuf.shape

    @pl.when(e == 0)
    def _():
        acc[...] = jnp.zeros_like(acc)

    act = _dot(down_ref[...], h2t_ref[...])
    rows_per_chunk = ec // PEER_KEYS
    assert rows_per_chunk == SUBLANES
    for j in range(rows_per_chunk):
        rs = slice(PEER_KEYS * j, PEER_KEYS * (j + 1))
        for c in range(tb // LANES):
            cs = slice(LANES * c, LANES * (c + 1))
            w = jnp.zeros((PEER_KEYS, LANES), F32)
            for h in range(PEER_HEADS):
                hs = slice(PEER_KEYS * h, PEER_KEYS * (h + 1))
                i1s = pl.ds(pl.multiple_of(PEER_KEYS * h + e * SUBLANES, SUBLANES), SUBLANES)
                tau = tau_ref[i1s, cs][j:j + 1]
                e1 = e1_ref[i1s, cs][j:j + 1]
                w = w + jnp.where(s2_ref[hs, cs] >= tau, e2_ref[hs, cs], 0.0) * e1
            pbuf[rs, cs] = (w * _gelu(act[rs, cs])).astype(BF16)
    acc[...] += _dot(upt_ref[...], pbuf[...])

    @pl.when(e == pl.num_programs(1) - 1)
    def _():
        y = acc[...].T
        out_ref[...] = xmid_ref[...] + g2_ref[0] * _rms(y, nw_ref[...])


def _expert(x_mid, gate2, norm_w, h2t, tau, e1, s2, e2, down, upt, seq):
    t, d = x_mid.shape
    tb = EXPERT_TOKENS
    ec = EXPERT_CHUNK
    per_seq = seq // tb
    n = PEER_HEADS * PEER_KEYS
    sspec = pl.BlockSpec((n, tb), lambda i, e: (0, i))
    return pl.pallas_call(
        _expert_kernel,
        grid=(t // tb, PEER_EXPERTS // ec),
        in_specs=[
            pl.BlockSpec((tb, d), lambda i, e: (i, 0)),
            pl.BlockSpec((1, 1, d), lambda i, e: (i // per_seq, 0, 0)),
            pl.BlockSpec((1, d), lambda i, e: (0, 0)),
            pl.BlockSpec((d, tb), lambda i, e: (0, i)),
            sspec, sspec, sspec, sspec,
            pl.BlockSpec((ec, d), lambda i, e: (e, 0)),
            pl.BlockSpec((d, ec), lambda i, e: (0, e)),
        ],
        out_specs=pl.BlockSpec((tb, d), lambda i, e: (i, 0)),
        out_shape=jax.ShapeDtypeStruct((t, d), F32),
        scratch_shapes=[pltpu.VMEM((d, tb), F32), pltpu.VMEM((ec, tb), BF16)],
        compiler_params=pltpu.CompilerParams(
            dimension_semantics=("arbitrary", "arbitrary"), vmem_limit_bytes=VMEM_LIMIT),
        name="peer_expert",
    )(x_mid, gate2, norm_w, h2t, tau, e1, s2, e2, down, upt)


def _block_diag4(w):
    hd = w.shape[-1]
    w4 = w.reshape(LRU_HEADS // 4, 4, hd, hd)
    eye = jnp.eye(4, dtype=w.dtype)
    return jnp.einsum("kmij,mn->kminj", w4, eye).reshape(LRU_HEADS // 4, 4 * hd, 4 * hd)


def _head_expander(width):
    head = lax.broadcasted_iota(jnp.int32, (DT_PAD, SSD_HEADS * width), 0)
    col = lax.broadcasted_iota(jnp.int32, (DT_PAD, SSD_HEADS * width), 1)
    r = (col // width == head).astype(BF16)
    return jnp.concatenate([r, r, r], axis=0)


def _layer(x2, mod, bsz, seq, lw):
    d = D_MODEL
    mods = [mod[:, d * k:d * (k + 1)].reshape(bsz, 1, d) for k in range(6)]
    shift1, scale1, gate1, shift2, scale2, gate2 = mods
    o1 = LRU_WIDTH
    o2 = o1 + SSD_INNER
    o3 = o2 + SSD_CONV_DIM
    o4 = o3 + SSD_HEADS
    o5 = o4 + D_MODEL
    w_in = lw["w_in"]
    w_dt = jnp.pad(w_in[:, o3:o4], ((0, 0), (0, DT_PAD - SSD_HEADS)))
    w1 = jnp.concatenate([w_in[:, :o1], w_in[:, o2:o3], w_dt], axis=1).astype(BF16)
    w_zg = jnp.concatenate([w_in[:, o1:o2], w_in[:, o4:o5], w_in[:, o5:]], axis=1).astype(BF16)

    def row(a):
        return a.reshape(1, -1)

    wg = jnp.concatenate([_block_diag4(lw["lru_wa"]), _block_diag4(lw["lru_wx"])], axis=2).astype(BF16)
    pad_h = (0, DT_PAD - SSD_HEADS)
    p = {
        "lru_conv_w": lw["lru_conv_w"], "lru_conv_b": row(lw["lru_conv_b"]), "lru_wg": wg,
        "lru_ba": row(lw["lru_ba"]), "lru_bx": row(lw["lru_bx"]), "lru_lambda": row(lw["lru_lambda"]),
        "ssd_conv_w": lw["ssd_conv_w"], "ssd_conv_b": row(lw["ssd_conv_b"]),
        "ssd_dt_bias": row(jnp.pad(lw["ssd_dt_bias"], pad_h)), "ssd_a_log": row(jnp.pad(lw["ssd_a_log"], pad_h)),
        "r64": _head_expander(SSD_HEAD_DIM), "r128": _head_expander(LANES),
        "ssd_dx": row(jnp.repeat(lw["ssd_d"], SSD_HEAD_DIM)),
        "norm_pre_mix": row(lw["norm_pre_mix"]), "norm_post_mix": row(lw["norm_post_mix"]),
        "norm_pre_ffn": row(lw["norm_pre_ffn"]), "ssd_norm_w": row(lw["ssd_norm_w"]),
        "w_zg": w_zg, "w_lru_out": lw["w_lru_out"].astype(BF16), "w_ssd_out": lw["w_ssd_out"].astype(BF16),
        "w_o": lw["w_o"].astype(BF16),
    }
    lru_x, xbc, dt = _inproj(x2, scale1, shift1, p["norm_pre_mix"], w1, seq)
    y_lru, y_ssd = _mixer(lru_x, xbc, dt, p, bsz, seq)
    x_mid, h2t = _outproj(x2, y_lru, y_ssd, [scale1, shift1, gate1, scale2, shift2], p, seq)
    wqt = lw["peer_wq"].T.astype(BF16)
    keys = lw["peer_sub_keys"].reshape(2 * PEER_HEADS, PEER_KEYS, PEER_HALF).astype(BF16)
    tau, e1, s2, e2 = _score(h2t, wqt, keys)
    down = lw["peer_down"].astype(BF16)
    upt = lw["peer_up"].T.astype(BF16)
    return _expert(x_mid, gate2, row(lw["norm_post_ffn"]), h2t, tau, e1, s2, e2, down, upt, seq)


def kernel(x, c, w_ada, b_ada, norm_pre_mix, norm_post_mix, norm_pre_ffn, norm_post_ffn, w_in, lru_conv_w, lru_conv_b, lru_wa, lru_ba, lru_wx, lru_bx, lru_lambda, ssd_conv_w, ssd_conv_b, ssd_dt_bias, ssd_a_log, ssd_d, ssd_norm_w, w_lru_out, w_ssd_out, w_o, peer_wq, peer_sub_keys, peer_down, peer_up):
    bsz, seq, d = x.shape
    stacked = dict(
        norm_pre_mix=norm_pre_mix, norm_post_mix=norm_post_mix, norm_pre_ffn=norm_pre_ffn,
        norm_post_ffn=norm_post_ffn, w_in=w_in, lru_conv_w=lru_conv_w, lru_conv_b=lru_conv_b,
        lru_wa=lru_wa, lru_ba=lru_ba, lru_wx=lru_wx, lru_bx=lru_bx, lru_lambda=lru_lambda,
        ssd_conv_w=ssd_conv_w, ssd_conv_b=ssd_conv_b, ssd_dt_bias=ssd_dt_bias, ssd_a_log=ssd_a_log,
        ssd_d=ssd_d, ssd_norm_w=ssd_norm_w, w_lru_out=w_lru_out, w_ssd_out=w_ssd_out, w_o=w_o,
        peer_wq=peer_wq, peer_sub_keys=peer_sub_keys, peer_down=peer_down, peer_up=peer_up)
    x2 = x.reshape(bsz * seq, d)
    for l in range(w_ada.shape[0]):
        mod = _adaln(c, w_ada[l], b_ada[l])
        x2 = _layer(x2, mod, bsz, seq, {k: v[l] for k, v in stacked.items()})
    return x2.reshape(bsz, seq, d)
```
